```python
import math
import jax, jax.numpy as jnp
from jax import lax
import numpy as np

D_MODEL = 1024
BATCH = 2
SEQ = 8192
DEPTH = 2

D_MIX = 3 * D_MODEL // 2
W_A = D_MIX // 3
W_B = D_MIX // 3
W_C = D_MIX - W_A - W_B
HEAD_DIM = 64
N_HEADS_A = W_A // HEAD_DIM
N_HEADS_B = W_B // HEAD_DIM
POOL_WINDOWS = (2, 4, 8, 16)
N_POOL_GROUPS = len(POOL_WINDOWS)
POOL_GROUP_DIM = W_C // N_POOL_GROUPS
CONV_A_WIDTH = 3
CONV_B_WIDTH = 31
D_IN = 4 * W_A + 3 * W_B + 2 * W_C
DEEPNORM_ALPHA = (2.0 * DEPTH) ** 0.25
DEEPNORM_BETA = (8.0 * DEPTH) ** -0.25
LN_EPS = 1e-5

kernel_name = "hybrid_conv_pool_deepnorm_trunk"


def layer_norm(x, g, b):
    xf = x.astype(jnp.float32)
    mu = jnp.mean(xf, axis=-1, keepdims=True)
    xc = xf - mu
    var = jnp.mean(xc * xc, axis=-1, keepdims=True)
    y = xc * lax.rsqrt(var + LN_EPS) * g.astype(jnp.float32) + b.astype(jnp.float32)
    return y.astype(x.dtype)


def causal_depthwise_conv(u, w, b):
    k, c = w.shape
    y = lax.conv_general_dilated(
        u, w[:, None, :].astype(u.dtype),
        window_strides=(1,), padding=[(k - 1, 0)],
        dimension_numbers=("NWC", "WIO", "NWC"),
        feature_group_count=c)
    return y + b.astype(u.dtype)


def multiscale_causal_pool(u):
    t = u.shape[1]
    uf = u.astype(jnp.float32)
    cs = jnp.cumsum(uf, axis=1)
    pos = jnp.arange(t, dtype=jnp.float32)[None, :, None]
    outs = []
    for g, w in enumerate(POOL_WINDOWS):
        sl = slice(g * POOL_GROUP_DIM, (g + 1) * POOL_GROUP_DIM)
        cs_g = cs[..., sl]
        cs_shift = jnp.pad(cs_g[:, : t - w], ((0, 0), (w, 0), (0, 0)))
        count = jnp.minimum(pos + 1.0, float(w))
        mean = (cs_g - cs_shift) / count
        outs.append(mean - uf[..., sl])
    return jnp.stack(outs, axis=2).astype(u.dtype)


def hybrid_layer(x, w_in, conv_a_w, conv_a_b, conv_b_w, conv_b_b, ln_b_g, ln_b_b,
                 pool_w, pool_b, pool_scale, w_out, ln_g, ln_b):
    bsz, t, _ = x.shape
    h = jnp.einsum("btd,de->bte", x, w_in)
    splits = np.cumsum([W_A, W_A, W_A, W_A, W_B, W_B, W_B, W_C])
    a_bg, a_cg, a_v, a_z, b_v, b_g, b_z, c_u, c_z = jnp.split(h, splits, axis=-1)

    y_a = a_bg * causal_depthwise_conv(a_cg * a_v, conv_a_w, conv_a_b)
    y_a = y_a * jax.nn.silu(a_z)

    u_b = b_v * jax.nn.sigmoid(b_g)
    u_b = causal_depthwise_conv(u_b, conv_b_w, conv_b_b)
    u_b = jax.nn.silu(layer_norm(u_b, ln_b_g, ln_b_b))
    y_b = u_b * jax.nn.silu(b_z)

    p = multiscale_causal_pool(c_u)
    p = jnp.einsum("btgc,gcd->btgd", p, pool_w) + pool_b
    y_c = p.reshape(bsz, t, W_C) * pool_scale
    y_c = y_c * jax.nn.silu(c_z)

    y = jnp.concatenate([y_a, y_b, y_c], axis=-1)
    out = jnp.einsum("bte,ed->btd", y, w_out)
    return layer_norm(DEEPNORM_ALPHA * x + out, ln_g, ln_b)


def setup_inputs(seed: int = 0) -> dict:
    key = jax.random.key(seed)
    ks = jax.random.split(key, 16)
    f32 = jnp.float32
    nrm = lambda k, s, sc: jax.random.normal(k, s, f32) * sc
    return {
        "x": jax.random.normal(ks[0], (BATCH, SEQ, D_MODEL), f32),
        "w_in": nrm(ks[1], (DEPTH, D_MODEL, D_IN), D_MODEL ** -0.5),
        "conv_a_w": nrm(ks[2], (DEPTH, CONV_A_WIDTH, W_A), CONV_A_WIDTH ** -0.5),
        "conv_a_b": nrm(ks[3], (DEPTH, W_A), 0.02),
        "conv_b_w": nrm(ks[4], (DEPTH, CONV_B_WIDTH, W_B), CONV_B_WIDTH ** -0.5),
        "conv_b_b": nrm(ks[5], (DEPTH, W_B), 0.02),
        "ln_b_g": 1.0 + nrm(ks[6], (DEPTH, W_B), 0.02),
        "ln_b_b": nrm(ks[7], (DEPTH, W_B), 0.02),
        "pool_w": nrm(ks[8], (DEPTH, N_POOL_GROUPS, POOL_GROUP_DIM, POOL_GROUP_DIM), POOL_GROUP_DIM ** -0.5),
        "pool_b": nrm(ks[9], (DEPTH, N_POOL_GROUPS, POOL_GROUP_DIM), 0.02),
        "pool_scale": 1.0 + nrm(ks[10], (DEPTH, W_C), 0.02),
        "w_out": nrm(ks[11], (DEPTH, D_MIX, D_MODEL), DEEPNORM_BETA * D_MIX ** -0.5),
        "ln_g": 1.0 + nrm(ks[12], (DEPTH, D_MODEL), 0.02),
        "ln_b": nrm(ks[13], (DEPTH, D_MODEL), 0.02),
    }


def reference(x, w_in, conv_a_w, conv_a_b, conv_b_w, conv_b_b, ln_b_g, ln_b_b,
              pool_w, pool_b, pool_scale, w_out, ln_g, ln_b):
    for l in range(DEPTH):
        x = hybrid_layer(x, w_in[l], conv_a_w[l], conv_a_b[l], conv_b_w[l], conv_b_b[l],
                         ln_b_g[l], ln_b_b[l], pool_w[l], pool_b[l], pool_scale[l],
                         w_out[l], ln_g[l], ln_b[l])
    return x
```

```python
import functools
import math

import jax
import jax.numpy as jnp
from jax import lax
from jax.experimental import pallas as pl
from jax.experimental.pallas import tpu as pltpu

LN_EPS = 1e-5
POOL_WINDOWS = (2, 4, 8, 16)
LANES = 128
SUBLANES = 8
TILE_T = 512
ROW_CHUNK = 32
VMEM_LIMIT_BYTES = 56 * 1024 * 1024


def _round_up(n, m):
    return (n + m - 1) // m * m


def _silu(v):
    return v / (1.0 + jnp.exp(-v))


def _sigmoid(v):
    return 1.0 / (1.0 + jnp.exp(-v))


def _layer_norm_rows(v, g, b):
    mu = jnp.mean(v, axis=-1, keepdims=True)
    vc = v - mu
    var = jnp.mean(vc * vc, axis=-1, keepdims=True)
    return vc * lax.rsqrt(var + LN_EPS) * g + b


def _layer_kernel(x_ref, win_ref, caw_ref, cab_ref, cbw_ref, cbb_ref, lnbg_ref, lnbb_ref,
                  pw_ref, pb_ref, ps_ref, wout_ref, lng_ref, lnb_ref,
                  o_ref,
                  h_ref, y_ref, acc_ref, ua_ref, ub_ref, uc_ref,
                  *, width, ka, kb, halo_a, halo_b, halo_c, alpha):
    w = width
    t = x_ref.shape[1]
    j = pl.program_id(1)
    n_chunks = t // ROW_CHUNK

    @pl.when(j == 0)
    def _():
        ua_ref[0:halo_a, :] = jnp.zeros((halo_a, w), jnp.float32)
        ub_ref[0:halo_b, :] = jnp.zeros((halo_b, w), jnp.float32)
        uc_ref[0:halo_c, :] = jnp.zeros((halo_c, w), jnp.float32)

    h_ref[...] = jnp.dot(x_ref[0].astype(jnp.bfloat16), win_ref[...],
                         preferred_element_type=jnp.float32)

    t0 = j * t

    def mix_chunk(c, carry):
        r0 = pl.multiple_of(c * ROW_CHUNK, ROW_CHUNK)
        rows = pl.ds(r0, ROW_CHUNK)

        def hcol(i):
            return h_ref[rows, i * w:(i + 1) * w]

        ua_ref[pl.ds(halo_a + r0, ROW_CHUNK), :] = hcol(1) * hcol(2)
        ua_win = ua_ref.at[pl.ds(r0, halo_a + ROW_CHUNK), :]
        conv_a = jnp.broadcast_to(cab_ref[...], (ROW_CHUNK, w))
        for k in range(ka):
            conv_a = conv_a + caw_ref[k:k + 1, :] * ua_win[pl.ds(halo_a - (ka - 1) + k, ROW_CHUNK), :]
        y_a = hcol(0) * conv_a * _silu(hcol(3))
        y_ref[rows, 0:w] = y_a.astype(jnp.bfloat16)

        ub_ref[pl.ds(halo_b + r0, ROW_CHUNK), :] = hcol(4) * _sigmoid(hcol(5))
        ub_win = ub_ref.at[pl.ds(r0, halo_b + ROW_CHUNK), :]
        parts = []
        for g in range(w // LANES):
            lanes = slice(g * LANES, (g + 1) * LANES)
            conv_b = jnp.broadcast_to(cbb_ref[:, lanes], (ROW_CHUNK, LANES))
            for k in range(kb):
                conv_b = conv_b + cbw_ref[k:k + 1, lanes] * ub_win[pl.ds(halo_b - (kb - 1) + k, ROW_CHUNK), lanes]
            parts.append(conv_b)
        conv_b = jnp.concatenate(parts, axis=-1)
        u_b = _silu(_layer_norm_rows(conv_b, lnbg_ref[...], lnbb_ref[...]))
        y_b = u_b * _silu(hcol(6))
        y_ref[rows, w:2 * w] = y_b.astype(jnp.bfloat16)

        c_u = hcol(7)
        uc_ref[pl.ds(halo_c + r0, ROW_CHUNK), :] = c_u
        uc_win = uc_ref.at[pl.ds(r0, halo_c + ROW_CHUNK), :]
        pos = (t0 + r0 + lax.broadcasted_iota(jnp.int32, (ROW_CHUNK, LANES), 0)).astype(jnp.float32)
        pooled = []
        for g, win in enumerate(POOL_WINDOWS):
            lanes = slice(g * LANES, (g + 1) * LANES)
            s = c_u[:, lanes]
            for d in range(1, win):
                s = s + uc_win[pl.ds(halo_c - d, ROW_CHUNK), lanes]
            count = jnp.minimum(pos + 1.0, float(win))
            p = (s / count - c_u[:, lanes]).astype(jnp.bfloat16)
            pooled.append(jnp.dot(p, pw_ref[g], preferred_element_type=jnp.float32))
        pm = jnp.concatenate(pooled, axis=-1) + pb_ref[...]
        y_c = pm * ps_ref[...] * _silu(hcol(8))
        y_ref[rows, 2 * w:3 * w] = y_c.astype(jnp.bfloat16)
        return carry

    lax.fori_loop(0, n_chunks, mix_chunk, 0)

    ua_ref[0:halo_a, :] = ua_ref[t:t + halo_a, :]
    ub_ref[0:halo_b, :] = ub_ref[t:t + halo_b, :]
    uc_ref[0:halo_c, :] = uc_ref[t:t + halo_c, :]

    acc_ref[...] = jnp.dot(y_ref[...], wout_ref[...], preferred_element_type=jnp.float32)

    def norm_chunk(c, carry):
        r0 = pl.multiple_of(c * ROW_CHUNK, ROW_CHUNK)
        rows = pl.ds(r0, ROW_CHUNK)
        z = alpha * x_ref[0, rows, :] + acc_ref[rows, :]
        o_ref[0, rows, :] = _layer_norm_rows(z, lng_ref[...], lnb_ref[...])
        return carry

    lax.fori_loop(0, n_chunks, norm_chunk, 0)


def _layer(x, w_in, caw, cab, cbw, cbb, lnbg, lnbb, pw, pb, ps, w_out, lng, lnb, *, alpha):
    bsz, seq, d_model = x.shape
    d_in = w_in.shape[1]
    width = d_in // 9
    ka, kb = caw.shape[0], cbw.shape[0]
    assert seq % TILE_T == 0 and TILE_T % ROW_CHUNK == 0
    assert width == len(POOL_WINDOWS) * LANES and w_out.shape == (3 * width, d_model)
    halo_a = _round_up(ka - 1, SUBLANES)
    halo_b = _round_up(kb - 1, SUBLANES)
    halo_c = _round_up(max(POOL_WINDOWS) - 1, SUBLANES)

    def whole(a):
        return pl.BlockSpec(a.shape, lambda b, j, _n=a.ndim: (0,) * _n,
                            pipeline_mode=pl.Buffered(1))

    row = lambda v: v.reshape(1, -1)
    params = (w_in.astype(jnp.bfloat16), caw, row(cab), cbw, row(cbb), row(lnbg), row(lnbb),
              pw.astype(jnp.bfloat16), row(pb), row(ps), w_out.astype(jnp.bfloat16),
              row(lng), row(lnb))
    tile = pl.BlockSpec((1, TILE_T, d_model), lambda b, j: (b, j, 0))
    kern = functools.partial(_layer_kernel, width=width, ka=ka, kb=kb, halo_a=halo_a,
                             halo_b=halo_b, halo_c=halo_c, alpha=alpha)
    return pl.pallas_call(
        kern,
        grid=(bsz, seq // TILE_T),
        in_specs=[tile] + [whole(p) for p in params],
        out_specs=tile,
        out_shape=jax.ShapeDtypeStruct(x.shape, x.dtype),
        scratch_shapes=[
            pltpu.VMEM((TILE_T, d_in), jnp.float32),
            pltpu.VMEM((TILE_T, 3 * width), jnp.bfloat16),
            pltpu.VMEM((TILE_T, d_model), jnp.float32),
            pltpu.VMEM((halo_a + TILE_T, width), jnp.float32),
            pltpu.VMEM((halo_b + TILE_T, width), jnp.float32),
            pltpu.VMEM((halo_c + TILE_T, width), jnp.float32),
        ],
        compiler_params=pltpu.CompilerParams(
            dimension_semantics=("arbitrary", "arbitrary"),
            vmem_limit_bytes=VMEM_LIMIT_BYTES),
        name="hybrid_layer",
    )(x, *params)


def kernel(x, w_in, conv_a_w, conv_a_b, conv_b_w, conv_b_b, ln_b_g, ln_b_b, pool_w, pool_b,
           pool_scale, w_out, ln_g, ln_b):
    depth = w_in.shape[0]
    alpha = (2.0 * depth) ** 0.25
    for l in range(depth):
        x = _layer(x, w_in[l], conv_a_w[l], conv_a_b[l], conv_b_w[l], conv_b_b[l], ln_b_g[l],
                   ln_b_b[l], pool_w[l], pool_b[l], pool_scale[l], w_out[l], ln_g[l], ln_b[l],
                   alpha=alpha)
    return x
```

```python
import functools

import jax
import jax.numpy as jnp
from jax import lax
from jax.experimental import pallas as pl
from jax.experimental.pallas import tpu as pltpu

LN_EPS = 1e-5
POOL_WINDOWS = (2, 4, 8, 16)
LANES = 128
SUBLANES = 8
TILE_T = 512
ROW_CHUNK = 32
VMEM_LIMIT_BYTES = 56 * 1024 * 1024


def _round_up(n, m):
    return (n + m - 1) // m * m


def _silu(v):
    return v / (1.0 + jnp.exp(-v))


def _sigmoid(v):
    return 1.0 / (1.0 + jnp.exp(-v))


def _layer_norm_rows(v, g, b):
    mu = jnp.mean(v, axis=-1, keepdims=True)
    vc = v - mu
    var = jnp.mean(vc * vc, axis=-1, keepdims=True)
    return vc * lax.rsqrt(var + LN_EPS) * g + b


def _layer_kernel(x_ref, win_ref, caw_ref, cab_ref, cbw_ref, cbb_ref, lnbg_ref, lnbb_ref,
                  pw_ref, pb_ref, ps_ref, wout_ref, lng_ref, lnb_ref,
                  o_ref,
                  h_ref, y_ref, acc_ref, ua_ref, ub_ref, uc_ref,
                  *, width, ka, kb, halo_a, halo_b, halo_c, alpha):
    w = width
    t = x_ref.shape[1]
    j = pl.program_id(1)
    n_chunks = t // ROW_CHUNK
    n_groups = w // LANES

    @pl.when(j == 0)
    def _():
        ua_ref[:, 0:halo_a, :] = jnp.zeros((n_groups, halo_a, LANES), jnp.float32)
        ub_ref[:, 0:halo_b, :] = jnp.zeros((n_groups, halo_b, LANES), jnp.float32)
        uc_ref[:, 0:halo_c, :] = jnp.zeros((n_groups, halo_c, LANES), jnp.float32)

    h_ref[...] = jnp.dot(x_ref[0].astype(jnp.bfloat16), win_ref[...],
                         preferred_element_type=jnp.float32)

    t0 = j * t

    def mix_chunk(c, carry):
        r0 = pl.multiple_of(c * ROW_CHUNK, ROW_CHUNK)
        rows = pl.ds(r0, ROW_CHUNK)
        pos = (t0 + r0 + lax.broadcasted_iota(jnp.int32, (ROW_CHUNK, LANES), 0)).astype(jnp.float32)

        def shifted(ref, g, halo, back):
            return ref[g, pl.ds(halo + r0 - back, ROW_CHUNK), :]

        conv_b_parts = []
        for g in range(n_groups):
            lanes = slice(g * LANES, (g + 1) * LANES)

            def hcol(i, _lanes=lanes):
                return h_ref[rows, i * w + _lanes.start:i * w + _lanes.stop]

            ua_ref[g, pl.ds(halo_a + r0, ROW_CHUNK), :] = hcol(1) * hcol(2)
            conv_a = jnp.broadcast_to(cab_ref[:, lanes], (ROW_CHUNK, LANES))
            for k in range(ka):
                conv_a = conv_a + caw_ref[k:k + 1, lanes] * shifted(ua_ref, g, halo_a, ka - 1 - k)
            y_ref[rows, lanes] = (hcol(0) * conv_a * _silu(hcol(3))).astype(jnp.bfloat16)

            ub_ref[g, pl.ds(halo_b + r0, ROW_CHUNK), :] = hcol(4) * _sigmoid(hcol(5))
            conv_b = jnp.broadcast_to(cbb_ref[:, lanes], (ROW_CHUNK, LANES))
            for k in range(kb):
                conv_b = conv_b + cbw_ref[k:k + 1, lanes] * shifted(ub_ref, g, halo_b, kb - 1 - k)
            conv_b_parts.append(conv_b)

            c_u = hcol(7)
            uc_ref[g, pl.ds(halo_c + r0, ROW_CHUNK), :] = c_u
            win = POOL_WINDOWS[g]
            s = c_u
            for d in range(1, win):
                s = s + shifted(uc_ref, g, halo_c, d)
            count = jnp.minimum(pos + 1.0, float(win))
            p = (s / count - c_u).astype(jnp.bfloat16)
            pm = jnp.dot(p, pw_ref[g], preferred_element_type=jnp.float32) + pb_ref[:, lanes]
            y_c = pm * ps_ref[:, lanes] * _silu(hcol(8))
            y_ref[rows, 2 * w + g * LANES:2 * w + (g + 1) * LANES] = y_c.astype(jnp.bfloat16)

        conv_b = jnp.concatenate(conv_b_parts, axis=-1)
        u_b = _silu(_layer_norm_rows(conv_b, lnbg_ref[...], lnbb_ref[...]))
        y_b = u_b * _silu(h_ref[rows, 6 * w:7 * w])
        y_ref[rows, w:2 * w] = y_b.astype(jnp.bfloat16)
        return carry

    lax.fori_loop(0, n_chunks, mix_chunk, 0)

    for g in range(n_groups):
        ua_ref[g, 0:halo_a, :] = ua_ref[g, t:t + halo_a, :]
        ub_ref[g, 0:halo_b, :] = ub_ref[g, t:t + halo_b, :]
        uc_ref[g, 0:halo_c, :] = uc_ref[g, t:t + halo_c, :]

    acc_ref[...] = jnp.dot(y_ref[...], wout_ref[...], preferred_element_type=jnp.float32)

    def norm_chunk(c, carry):
        r0 = pl.multiple_of(c * ROW_CHUNK, ROW_CHUNK)
        rows = pl.ds(r0, ROW_CHUNK)
        z = alpha * x_ref[0, rows, :] + acc_ref[rows, :]
        o_ref[0, rows, :] = _layer_norm_rows(z, lng_ref[...], lnb_ref[...])
        return carry

    lax.fori_loop(0, n_chunks, norm_chunk, 0, unroll=4)


def _layer(x, w_in, caw, cab, cbw, cbb, lnbg, lnbb, pw, pb, ps, w_out, lng, lnb, *, alpha):
    bsz, seq, d_model = x.shape
    d_in = w_in.shape[1]
    width = d_in // 9
    ka, kb = caw.shape[0], cbw.shape[0]
    assert seq % TILE_T == 0 and TILE_T % ROW_CHUNK == 0
    assert width == len(POOL_WINDOWS) * LANES and w_out.shape == (3 * width, d_model)
    n_groups = width // LANES
    halo_a = _round_up(ka - 1, SUBLANES)
    halo_b = _round_up(kb - 1, SUBLANES)
    halo_c = _round_up(max(POOL_WINDOWS) - 1, SUBLANES)

    def whole(a):
        return pl.BlockSpec(a.shape, lambda b, j, _n=a.ndim: (0,) * _n,
                            pipeline_mode=pl.Buffered(1))

    row = lambda v: v.reshape(1, -1)
    params = (w_in.astype(jnp.bfloat16), caw, row(cab), cbw, row(cbb), row(lnbg), row(lnbb),
              pw.astype(jnp.bfloat16), row(pb), row(ps), w_out.astype(jnp.bfloat16),
              row(lng), row(lnb))
    tile = pl.BlockSpec((1, TILE_T, d_model), lambda b, j: (b, j, 0))
    kern = functools.partial(_layer_kernel, width=width, ka=ka, kb=kb, halo_a=halo_a,
                             halo_b=halo_b, halo_c=halo_c, alpha=alpha)
    return pl.pallas_call(
        kern,
        grid=(bsz, seq // TILE_T),
        in_specs=[tile] + [whole(p) for p in params],
        out_specs=tile,
        out_shape=jax.ShapeDtypeStruct(x.shape, x.dtype),
        scratch_shapes=[
            pltpu.VMEM((TILE_T, d_in), jnp.float32),
            pltpu.VMEM((TILE_T, 3 * width), jnp.bfloat16),
            pltpu.VMEM((TILE_T, d_model), jnp.float32),
            pltpu.VMEM((n_groups, halo_a + TILE_T, LANES), jnp.float32),
            pltpu.VMEM((n_groups, halo_b + TILE_T, LANES), jnp.float32),
            pltpu.VMEM((n_groups, halo_c + TILE_T, LANES), jnp.float32),
        ],
        compiler_params=pltpu.CompilerParams(
            dimension_semantics=("arbitrary", "arbitrary"),
            vmem_limit_bytes=VMEM_LIMIT_BYTES),
        name="hybrid_layer",
    )(x, *params)


def kernel(x, w_in, conv_a_w, conv_a_b, conv_b_w, conv_b_b, ln_b_g, ln_b_b, pool_w, pool_b,
           pool_scale, w_out, ln_g, ln_b):
    depth = w_in.shape[0]
    alpha = (2.0 * depth) ** 0.25
    for l in range(depth):
        x = _layer(x, w_in[l], conv_a_w[l], conv_a_b[l], conv_b_w[l], conv_b_b[l], ln_b_g[l],
                   ln_b_b[l], pool_w[l], pool_b[l], pool_scale[l], w_out[l], ln_g[l], ln_b[l],
                   alpha=alpha)
    return x
```

```python
import functools

import jax
import jax.numpy as jnp
from jax import lax
from jax.experimental import pallas as pl
from jax.experimental.pallas import tpu as pltpu

LN_EPS = 1e-5
POOL_WINDOWS = (2, 4, 8, 16)
LANES = 128
SUBLANES = 8
TILE_T = 512
ROW_CHUNK = 32
VMEM_LIMIT_BYTES = 56 * 1024 * 1024


def _round_up(n, m):
    return (n + m - 1) // m * m


def _silu(v):
    return v / (1.0 + jnp.exp(-v))


def _sigmoid(v):
    return 1.0 / (1.0 + jnp.exp(-v))


def _layer_norm_rows(v, g, b):
    mu = jnp.mean(v, axis=-1, keepdims=True)
    vc = v - mu
    var = jnp.mean(vc * vc, axis=-1, keepdims=True)
    return vc * lax.rsqrt(var + LN_EPS) * g + b


def _layer_kernel(x_ref, win_ref, caw_ref, cab_ref, cbw_ref, cbb_ref, lnbg_ref, lnbb_ref,
                  pw_ref, pb_ref, ps_ref, wout_ref, lng_ref, lnb_ref,
                  o_ref,
                  xb_ref, h_ref, y_ref, acc_ref, ua_ref, ub_ref, uc_ref,
                  *, width, ka, kb, halo_a, halo_b, halo_c, alpha):
    w = width
    t = x_ref.shape[1]
    j = pl.program_id(1)
    n_chunks = t // ROW_CHUNK
    n_groups = w // LANES

    @pl.when(j == 0)
    def _():
        ua_ref[:, 0:halo_a, :] = jnp.zeros((n_groups, halo_a, LANES), jnp.float32)
        ub_ref[:, 0:halo_b, :] = jnp.zeros((n_groups, halo_b, LANES), jnp.float32)
        uc_ref[:, 0:halo_c, :] = jnp.zeros((n_groups, halo_c, LANES), jnp.float32)

    xb_ref[...] = x_ref[0].astype(jnp.bfloat16)
    for c0, c1 in ((4 * w, 6 * w), (7 * w, 9 * w), (0, 4 * w), (6 * w, 7 * w)):
        h_ref[:, c0:c1] = jnp.dot(xb_ref[...], win_ref[:, c0:c1],
                                  preferred_element_type=jnp.float32)

    t0 = j * t

    def mix_chunk(c, carry):
        r0 = c * ROW_CHUNK
        rows = pl.ds(r0, ROW_CHUNK)
        pos = (t0 + r0 + lax.broadcasted_iota(jnp.int32, (ROW_CHUNK, LANES), 0)).astype(jnp.float32)

        def shifted(ref, g, halo, back):
            return ref[g, pl.ds(halo + r0 - back, ROW_CHUNK), :]

        conv_b_parts = []
        for g in range(n_groups):
            lanes = slice(g * LANES, (g + 1) * LANES)

            def hcol(i, _lanes=lanes):
                return h_ref[rows, i * w + _lanes.start:i * w + _lanes.stop]

            ua_ref[g, pl.ds(halo_a + r0, ROW_CHUNK), :] = hcol(1) * hcol(2)
            conv_a = jnp.broadcast_to(cab_ref[:, lanes], (ROW_CHUNK, LANES))
            for k in range(ka):
                conv_a = conv_a + caw_ref[k:k + 1, lanes] * shifted(ua_ref, g, halo_a, ka - 1 - k)
            y_ref[rows, lanes] = (hcol(0) * conv_a * _silu(hcol(3))).astype(jnp.bfloat16)

            ub_ref[g, pl.ds(halo_b + r0, ROW_CHUNK), :] = hcol(4) * _sigmoid(hcol(5))
            conv_b = jnp.broadcast_to(cbb_ref[:, lanes], (ROW_CHUNK, LANES))
            for k in range(kb):
                conv_b = conv_b + cbw_ref[k:k + 1, lanes] * shifted(ub_ref, g, halo_b, kb - 1 - k)
            conv_b_parts.append(conv_b)

            c_u = hcol(7)
            uc_ref[g, pl.ds(halo_c + r0, ROW_CHUNK), :] = c_u
            win = POOL_WINDOWS[g]
            s = c_u
            for d in range(1, win):
                s = s + shifted(uc_ref, g, halo_c, d)
            count = jnp.minimum(pos + 1.0, float(win))
            p = (s / count - c_u).astype(jnp.bfloat16)
            pm = jnp.dot(p, pw_ref[g], preferred_element_type=jnp.float32) + pb_ref[:, lanes]
            y_c = pm * ps_ref[:, lanes] * _silu(hcol(8))
            y_ref[rows, 2 * w + g * LANES:2 * w + (g + 1) * LANES] = y_c.astype(jnp.bfloat16)

        conv_b = jnp.concatenate(conv_b_parts, axis=-1)
        u_b = _silu(_layer_norm_rows(conv_b, lnbg_ref[...], lnbb_ref[...]))
        y_b = u_b * _silu(h_ref[rows, 6 * w:7 * w])
        y_ref[rows, w:2 * w] = y_b.astype(jnp.bfloat16)
        return carry

    for c in range(n_chunks):
        mix_chunk(c, 0)

    for g in range(n_groups):
        ua_ref[g, 0:halo_a, :] = ua_ref[g, t:t + halo_a, :]
        ub_ref[g, 0:halo_b, :] = ub_ref[g, t:t + halo_b, :]
        uc_ref[g, 0:halo_c, :] = uc_ref[g, t:t + halo_c, :]

    acc_ref[...] = jnp.dot(y_ref[...], wout_ref[...], preferred_element_type=jnp.float32)

    def norm_chunk(c, carry):
        r0 = pl.multiple_of(c * ROW_CHUNK, ROW_CHUNK)
        rows = pl.ds(r0, ROW_CHUNK)
        z = alpha * x_ref[0, rows, :] + acc_ref[rows, :]
        o_ref[0, rows, :] = _layer_norm_rows(z, lng_ref[...], lnb_ref[...])
        return carry

    lax.fori_loop(0, n_chunks, norm_chunk, 0, unroll=4)


def _layer(x, w_in, caw, cab, cbw, cbb, lnbg, lnbb, pw, pb, ps, w_out, lng, lnb, *, alpha):
    bsz, seq, d_model = x.shape
    d_in = w_in.shape[1]
    width = d_in // 9
    ka, kb = caw.shape[0], cbw.shape[0]
    assert seq % TILE_T == 0 and TILE_T % ROW_CHUNK == 0
    assert width == len(POOL_WINDOWS) * LANES and w_out.shape == (3 * width, d_model)
    n_groups = width // LANES
    halo_a = _round_up(ka - 1, SUBLANES)
    halo_b = _round_up(kb - 1, SUBLANES)
    halo_c = _round_up(max(POOL_WINDOWS) - 1, SUBLANES)

    def whole(a):
        return pl.BlockSpec(a.shape, lambda b, j, _n=a.ndim: (0,) * _n,
                            pipeline_mode=pl.Buffered(1))

    row = lambda v: v.reshape(1, -1)
    params = (w_in.astype(jnp.bfloat16), caw, row(cab), cbw, row(cbb), row(lnbg), row(lnbb),
              pw.astype(jnp.bfloat16), row(pb), row(ps), w_out.astype(jnp.bfloat16),
              row(lng), row(lnb))
    tile = pl.BlockSpec((1, TILE_T, d_model), lambda b, j: (b, j, 0))
    kern = functools.partial(_layer_kernel, width=width, ka=ka, kb=kb, halo_a=halo_a,
                             halo_b=halo_b, halo_c=halo_c, alpha=alpha)
    return pl.pallas_call(
        kern,
        grid=(bsz, seq // TILE_T),
        in_specs=[tile] + [whole(p) for p in params],
        out_specs=tile,
        out_shape=jax.ShapeDtypeStruct(x.shape, x.dtype),
        scratch_shapes=[
            pltpu.VMEM((TILE_T, d_model), jnp.bfloat16),
            pltpu.VMEM((TILE_T, d_in), jnp.float32),
            pltpu.VMEM((TILE_T, 3 * width), jnp.bfloat16),
            pltpu.VMEM((TILE_T, d_model), jnp.float32),
            pltpu.VMEM((n_groups, halo_a + TILE_T, LANES), jnp.float32),
            pltpu.VMEM((n_groups, halo_b + TILE_T, LANES), jnp.float32),
            pltpu.VMEM((n_groups, halo_c + TILE_T, LANES), jnp.float32),
        ],
        compiler_params=pltpu.CompilerParams(
            dimension_semantics=("arbitrary", "arbitrary"),
            vmem_limit_bytes=VMEM_LIMIT_BYTES),
        name="hybrid_layer",
    )(x, *params)


def kernel(x, w_in, conv_a_w, conv_a_b, conv_b_w, conv_b_b, ln_b_g, ln_b_b, pool_w, pool_b,
           pool_scale, w_out, ln_g, ln_b):
    depth = w_in.shape[0]
    alpha = (2.0 * depth) ** 0.25
    for l in range(depth):
        x = _layer(x, w_in[l], conv_a_w[l], conv_a_b[l], conv_b_w[l], conv_b_b[l], ln_b_g[l],
                   ln_b_b[l], pool_w[l], pool_b[l], pool_scale[l], w_out[l], ln_g[l], ln_b[l],
                   alpha=alpha)
    return x
```
